```python
import jax, jax.numpy as jnp
from jax import lax
import numpy as np

D_MODEL = 1024
BATCH = 4
SEQ = 4096
DEPTH = 1

CHUNK = 64
MEM_LEN = 256
CONV_WIDTH = D_MODEL // 2
CONV_K = 3
SB_HEADS = 8
SB_HEAD_DIM = 64
SB_WIDTH = SB_HEADS * SB_HEAD_DIM
MIX_WIDTH = CONV_WIDTH + SB_WIDTH
Q_BLOCK = 128
XA_HEADS = 4
XA_HEAD_DIM = D_MODEL // XA_HEADS
XA_WIDTH = XA_HEADS * XA_HEAD_DIM
N_EXPERTS = 32
TOP_K = 4
D_EXPERT = D_MODEL
SWIGLU_ALPHA = 1.702
SWIGLU_LIMIT = 7.0
MOE_BLOCK = 128
EPS = 1e-6

kernel_name = "hybrid_shortconv_stickbreaking_memxattn_moe"


def rms_norm(x, g):
    xf = x.astype(jnp.float32)
    y = xf * lax.rsqrt(jnp.mean(xf * xf, axis=-1, keepdims=True) + EPS)
    return (y * g.astype(jnp.float32)).astype(x.dtype)


def short_gated_conv(u, gate_b, gate_c, w_conv):
    v = gate_c * u
    y = lax.conv_general_dilated(
        v, w_conv[:, None, :].astype(v.dtype), window_strides=(1,),
        padding=[(CONV_K - 1, 0)], dimension_numbers=("NWC", "WIO", "NWC"),
        feature_group_count=v.shape[-1])
    return gate_b * y


def stick_breaking_attention(q, k, v):
    S, dh = q.shape[2], q.shape[3]
    scale = dh ** -0.5
    outs = []
    for q0 in range(0, S, Q_BLOCK):
        q1 = q0 + Q_BLOCK
        kb, vb = k[:, :, :q1], v[:, :, :q1]
        z = jnp.einsum("bhqd,bhkd->bhqk", q[:, :, q0:q1], kb).astype(jnp.float32) * scale
        t_pos = jnp.arange(q0, q1)[:, None]
        s_pos = jnp.arange(q1)[None, :]
        mask = s_pos < t_pos
        log_keep = jnp.where(mask, jax.nn.log_sigmoid(-z), 0.0)
        after = lax.cumsum(log_keep, axis=3, reverse=True) - log_keep
        a = jnp.where(mask, jnp.exp(jax.nn.log_sigmoid(z) + after), 0.0)
        outs.append(jnp.einsum("bhqk,bhkd->bhqd", a.astype(vb.dtype), vb))
    return jnp.concatenate(outs, axis=2)


def hybrid_mixer(h, w_in, w_conv, g_conv_out, g_sb_out, w_out):
    bsz, S, _ = h.shape
    proj = h @ w_in
    cw, sw = CONV_WIDTH, SB_WIDTH
    splits = [cw, 2 * cw, 3 * cw, 3 * cw + sw, 3 * cw + 2 * sw]
    u, gate_b, gate_c, q, k, v = jnp.split(proj, splits, axis=-1)
    y_conv = rms_norm(short_gated_conv(u, gate_b, gate_c, w_conv), g_conv_out)

    def heads(t):
        return t.reshape(bsz, S, SB_HEADS, SB_HEAD_DIM).transpose(0, 2, 1, 3)

    o = stick_breaking_attention(heads(q), heads(k), heads(v)).transpose(0, 2, 1, 3)
    y_sb = rms_norm(o, g_sb_out.reshape(SB_HEADS, SB_HEAD_DIM)).reshape(bsz, S, SB_WIDTH)
    return jnp.concatenate([y_conv, y_sb], axis=-1) @ w_out


def memory_cross_attention(h, m, w_xq, w_xkv, w_xo):
    bsz, S, _ = h.shape
    n_mem = m.shape[1]
    q = (h @ w_xq).reshape(bsz, S, XA_HEADS, XA_HEAD_DIM)
    kv = (m @ w_xkv).reshape(bsz, n_mem, 2, XA_HEADS, XA_HEAD_DIM)
    k, v = kv[:, :, 0], kv[:, :, 1]
    s = jnp.einsum("bqhd,bkhd->bhqk", q, k).astype(jnp.float32) * (XA_HEAD_DIM ** -0.5)
    p = jax.nn.softmax(s, axis=-1).astype(v.dtype)
    o = jnp.einsum("bhqk,bkhd->bqhd", p, v).reshape(bsz, S, XA_WIDTH)
    return o @ w_xo


def moe_ffn(h, w_router, b_router, w_gate_up, b_gate_up, w_down, b_down):
    n_tok, d = h.shape
    logits = (h @ w_router).astype(jnp.float32) + b_router.astype(jnp.float32)
    top_val, top_idx = lax.top_k(logits, TOP_K)
    gates = jax.nn.softmax(top_val, axis=-1).astype(h.dtype)
    n_assign = n_tok * TOP_K
    flat_e = top_idx.reshape(-1)
    flat_tok = jnp.arange(n_assign, dtype=jnp.int32) // TOP_K
    flat_g = gates.reshape(-1)
    order = jnp.argsort(flat_e)
    sorted_e = flat_e[order]
    counts = jnp.bincount(flat_e, length=N_EXPERTS)
    group_start = jnp.cumsum(counts) - counts
    padded = (counts + MOE_BLOCK - 1) // MOE_BLOCK * MOE_BLOCK
    padded_end = jnp.cumsum(padded)
    padded_start = padded_end - padded
    dest = padded_start[sorted_e] + jnp.arange(n_assign) - group_start[sorted_e]
    n_blocks = -(-n_assign // MOE_BLOCK) + N_EXPERTS
    n_rows = n_blocks * MOE_BLOCK
    row_tok = jnp.zeros((n_rows,), jnp.int32).at[dest].set(flat_tok[order])
    row_gate = jnp.zeros((n_rows,), h.dtype).at[dest].set(flat_g[order])
    block_e = jnp.minimum(
        jnp.searchsorted(padded_end, jnp.arange(n_blocks) * MOE_BLOCK, side="right"),
        N_EXPERTS - 1)
    xb = h[row_tok].reshape(n_blocks, MOE_BLOCK, d)

    def expert_block(args):
        xe, e = args
        gu = xe @ w_gate_up[e] + b_gate_up[e]
        g = jnp.minimum(gu[:, 0::2], SWIGLU_LIMIT)
        up = jnp.clip(gu[:, 1::2], -SWIGLU_LIMIT, SWIGLU_LIMIT)
        glu = g * jax.nn.sigmoid(g * SWIGLU_ALPHA)
        return ((up + 1.0) * glu) @ w_down[e] + b_down[e]

    yb = lax.map(expert_block, (xb, block_e)).reshape(n_rows, d)
    return jnp.zeros_like(h).at[row_tok].add(yb * row_gate[:, None])


def setup_inputs(seed: int = 0) -> dict:
    key = jax.random.key(seed)
    ks = jax.random.split(key, 24)
    f32 = jnp.float32
    L = DEPTH

    def nrm(k, shape, scale):
        return jax.random.normal(k, shape, f32) * scale

    def gain(k, shape):
        return 1.0 + 0.05 * jax.random.normal(k, shape, f32)

    return {
        "x": nrm(ks[0], (BATCH, SEQ, D_MODEL), 1.0),
        "mem": nrm(ks[1], (BATCH, MEM_LEN, D_MODEL), 1.0),
        "g_mix": gain(ks[2], (L, D_MODEL)),
        "w_in": nrm(ks[3], (L, D_MODEL, 3 * MIX_WIDTH), D_MODEL ** -0.5),
        "w_conv": nrm(ks[4], (L, CONV_K, CONV_WIDTH), CONV_K ** -0.5),
        "g_conv_out": gain(ks[5], (L, CONV_WIDTH)),
        "g_sb_out": gain(ks[6], (L, SB_WIDTH)),
        "w_out": nrm(ks[7], (L, MIX_WIDTH, D_MODEL), MIX_WIDTH ** -0.5),
        "g_xattn": gain(ks[8], (L, D_MODEL)),
        "g_mem": gain(ks[9], (L, D_MODEL)),
        "w_xq": nrm(ks[10], (L, D_MODEL, XA_WIDTH), D_MODEL ** -0.5),
        "w_xkv": nrm(ks[11], (L, D_MODEL, 2 * XA_WIDTH), D_MODEL ** -0.5),
        "w_xo": nrm(ks[12], (L, XA_WIDTH, D_MODEL), XA_WIDTH ** -0.5),
        "g_ffn": gain(ks[13], (L, D_MODEL)),
        "w_router": nrm(ks[14], (L, D_MODEL, N_EXPERTS), D_MODEL ** -0.5),
        "b_router": nrm(ks[15], (L, N_EXPERTS), 0.01),
        "w_gate_up": nrm(ks[16], (L, N_EXPERTS, D_MODEL, 2 * D_EXPERT), D_MODEL ** -0.5),
        "b_gate_up": nrm(ks[17], (L, N_EXPERTS, 2 * D_EXPERT), 0.01),
        "w_down": nrm(ks[18], (L, N_EXPERTS, D_EXPERT, D_MODEL), D_EXPERT ** -0.5),
        "b_down": nrm(ks[19], (L, N_EXPERTS, D_MODEL), 0.01),
        "g_final": gain(ks[20], (D_MODEL,)),
    }


def reference(x, mem, g_mix, w_in, w_conv, g_conv_out, g_sb_out, w_out, g_xattn, g_mem,
              w_xq, w_xkv, w_xo, g_ffn, w_router, b_router, w_gate_up, b_gate_up,
              w_down, b_down, g_final):
    bsz, S, d = x.shape
    for l in range(DEPTH):
        h = rms_norm(x, g_mix[l])
        x = x + hybrid_mixer(h, w_in[l], w_conv[l], g_conv_out[l], g_sb_out[l], w_out[l])
        h = rms_norm(x, g_xattn[l])
        m = rms_norm(mem, g_mem[l])
        x = x + memory_cross_attention(h, m, w_xq[l], w_xkv[l], w_xo[l])
        h = rms_norm(x, g_ffn[l]).reshape(bsz * S, d)
        x = x + moe_ffn(h, w_router[l], b_router[l], w_gate_up[l], b_gate_up[l],
                        w_down[l], b_down[l]).reshape(bsz, S, d)
    return rms_norm(x, g_final)
```

```python
import functools

import jax
import jax.numpy as jnp
from jax import lax
from jax.experimental import pallas as pl
from jax.experimental.pallas import tpu as pltpu

F32 = jnp.float32
BF16 = jnp.bfloat16

EPS = 1e-6
CONV_WIDTH = 512
SB_HEAD_DIM = 64
SB_WIDTH = 512
XA_HEADS = 4
N_EXPERTS = 32
TOP_K = 4
SWIGLU_ALPHA = 1.702
SWIGLU_LIMIT = 7.0

LANES = 128
VMEM_LIMIT = 56 * 1024 * 1024

TM_IN = 512
TQ = 128
TM_MID = 512
TM_MOE = 256
TM_OUT = 256


def _rms(x, g):
    return x * lax.rsqrt(jnp.mean(x * x, axis=-1, keepdims=True) + EPS) * g


def _dot(a, b):
    return jnp.dot(a, b, preferred_element_type=F32)


def _mixer_in_kernel(x_ref, g_ref, w_ref, wc_ref, gc_ref,
                     ya_ref, q_ref, kt_ref, v_ref, vext_ref):
    tm = x_ref.shape[1]
    cw = CONV_WIDTH
    h = _rms(x_ref[0], g_ref[...])
    proj = _dot(h.astype(BF16), w_ref[...])
    u, gate_b, gate_c = proj[:, 0:cw], proj[:, cw:2 * cw], proj[:, 2 * cw:3 * cw]
    vv = gate_c * u

    @pl.when(pl.program_id(1) == 0)
    def _():
        vext_ref[0:8, :] = jnp.zeros((8, cw), F32)

    vext_ref[8:8 + tm, :] = vv
    w = wc_ref[...]
    y = w[2:3, :] * vv + w[1:2, :] * vext_ref[7:7 + tm, :] + w[0:1, :] * vext_ref[6:6 + tm, :]
    vext_ref[0:8, :] = vv[tm - 8:tm, :]
    ya_ref[0] = _rms(gate_b * y, gc_ref[...]).astype(BF16)

    o = 3 * cw
    q_ref[0] = (proj[:, o:o + SB_WIDTH] * (SB_HEAD_DIM ** -0.5)).astype(BF16)
    kt_ref[0] = proj[:, o + SB_WIDTH:o + 2 * SB_WIDTH].T.astype(BF16)
    v_ref[0] = proj[:, o + 2 * SB_WIDTH:o + 3 * SB_WIDTH].astype(BF16)


def _mixer_in(x, g_mix, w_in, w_conv, g_conv_out):
    bsz, seq, d = x.shape
    tm = TM_IN
    n_proj = w_in.shape[1]
    return pl.pallas_call(
        _mixer_in_kernel,
        grid=(bsz, seq // tm),
        in_specs=[
            pl.BlockSpec((1, tm, d), lambda b, i: (b, i, 0)),
            pl.BlockSpec((1, d), lambda b, i: (0, 0)),
            pl.BlockSpec((d, n_proj), lambda b, i: (0, 0)),
            pl.BlockSpec((3, CONV_WIDTH), lambda b, i: (0, 0)),
            pl.BlockSpec((1, CONV_WIDTH), lambda b, i: (0, 0)),
        ],
        out_specs=[
            pl.BlockSpec((1, tm, CONV_WIDTH), lambda b, i: (b, i, 0)),
            pl.BlockSpec((1, tm, SB_WIDTH), lambda b, i: (b, i, 0)),
            pl.BlockSpec((1, SB_WIDTH, tm), lambda b, i: (b, 0, i)),
            pl.BlockSpec((1, tm, SB_WIDTH), lambda b, i: (b, i, 0)),
        ],
        out_shape=[
            jax.ShapeDtypeStruct((bsz, seq, CONV_WIDTH), BF16),
            jax.ShapeDtypeStruct((bsz, seq, SB_WIDTH), BF16),
            jax.ShapeDtypeStruct((bsz, SB_WIDTH, seq), BF16),
            jax.ShapeDtypeStruct((bsz, seq, SB_WIDTH), BF16),
        ],
        scratch_shapes=[pltpu.VMEM((tm + 8, CONV_WIDTH), F32)],
        compiler_params=pltpu.CompilerParams(
            dimension_semantics=("arbitrary", "arbitrary"), vmem_limit_bytes=VMEM_LIMIT),
        name="mixer_in",
    )(x, g_mix.reshape(1, d), w_in.astype(BF16), w_conv, g_conv_out.reshape(1, CONV_WIDTH))


def _sb_kernel(q_ref, kt_ref, v_ref, g_ref, o_ref):
    tq = q_ref.shape[1]
    i = pl.program_id(2)
    q2 = q_ref[0]
    first = lax.broadcasted_iota(jnp.int32, (tq, LANES), 1) < SB_HEAD_DIM
    zq = jnp.zeros_like(q2)
    qs = (jnp.where(first, q2, zq), jnp.where(first, zq, q2))
    row = lax.broadcasted_iota(jnp.int32, (tq, tq), 0)
    col = lax.broadcasted_iota(jnp.int32, (tq, tq), 1)
    causal = col < row
    later = (row > col).astype(BF16)

    def tile(j, carry, masked):
        accs, runs = carry
        koff = pl.multiple_of(j * tq, tq)
        kt = kt_ref[0, :, pl.ds(koff, tq)]
        vb = v_ref[0, pl.ds(koff, tq), :]
        new_accs, new_runs = [], []
        for h in range(2):
            z = _dot(qs[h], kt)
            log_keep = -(jnp.maximum(z, 0.0) + jnp.log1p(jnp.exp(-jnp.abs(z))))
            lk = jnp.where(causal, log_keep, 0.0) if masked else log_keep
            hi = lk.astype(BF16)
            lo = (lk - hi.astype(F32)).astype(BF16)
            after = _dot(hi, later) + _dot(lo, later) + runs[h]
            a = jnp.exp(z + log_keep + after)
            if masked:
                a = jnp.where(causal, a, 0.0)
            new_accs.append(accs[h] + _dot(a.astype(BF16), vb))
            new_runs.append(runs[h] + jnp.sum(lk, axis=1, keepdims=True))
        return tuple(new_accs), tuple(new_runs)

    zacc = jnp.zeros((tq, LANES), F32)
    zrun = jnp.zeros((tq, 1), F32)
    carry = tile(i, ((zacc, zacc), (zrun, zrun)), True)
    accs, _ = lax.fori_loop(0, i, lambda jj, c: tile(i - 1 - jj, c, False), carry)

    o = jnp.where(first, accs[0], accs[1])
    o2 = o * o
    s0 = jnp.sum(jnp.where(first, o2, 0.0), axis=-1, keepdims=True)
    s1 = jnp.sum(jnp.where(first, 0.0, o2), axis=-1, keepdims=True)
    ms = jnp.where(first, s0, s1) * (1.0 / SB_HEAD_DIM)
    o_ref[0] = (o * lax.rsqrt(ms + EPS) * g_ref[...]).astype(BF16)


def _sb_attention(q, kt, v, g_sb_out):
    bsz, seq, width = q.shape
    n_pairs = width // LANES
    return pl.pallas_call(
        _sb_kernel,
        grid=(bsz, n_pairs, seq // TQ),
        in_specs=[
            pl.BlockSpec((1, TQ, LANES), lambda b, p, i: (b, i, p)),
            pl.BlockSpec((1, LANES, seq), lambda b, p, i: (b, p, 0)),
            pl.BlockSpec((1, seq, LANES), lambda b, p, i: (b, 0, p)),
            pl.BlockSpec((1, LANES), lambda b, p, i: (0, p)),
        ],
        out_specs=pl.BlockSpec((1, TQ, LANES), lambda b, p, i: (b, i, p)),
        out_shape=jax.ShapeDtypeStruct((bsz, seq, width), BF16),
        compiler_params=pltpu.CompilerParams(
            dimension_semantics=("arbitrary", "arbitrary", "arbitrary"),
            vmem_limit_bytes=VMEM_LIMIT),
        name="sb_attention",
    )(q, kt, v, g_sb_out.reshape(1, width))


def _mem_kv_kernel(m_ref, g_ref, w_ref, kt_ref, v_ref):
    d = m_ref.shape[2]
    kv = _dot(_rms(m_ref[0], g_ref[...]).astype(BF16), w_ref[...])
    kt_ref[0] = kv[:, 0:d].T.astype(BF16)
    v_ref[0] = kv[:, d:2 * d].astype(BF16)


def _mem_kv(mem, g_mem, w_xkv):
    bsz, n_mem, d = mem.shape
    return pl.pallas_call(
        _mem_kv_kernel,
        grid=(bsz,),
        in_specs=[
            pl.BlockSpec((1, n_mem, d), lambda b: (b, 0, 0)),
            pl.BlockSpec((1, d), lambda b: (0, 0)),
            pl.BlockSpec((d, 2 * d), lambda b: (0, 0)),
        ],
        out_specs=[
            pl.BlockSpec((1, d, n_mem), lambda b: (b, 0, 0)),
            pl.BlockSpec((1, n_mem, d), lambda b: (b, 0, 0)),
        ],
        out_shape=[
            jax.ShapeDtypeStruct((bsz, d, n_mem), BF16),
            jax.ShapeDtypeStruct((bsz, n_mem, d), BF16),
        ],
        compiler_params=pltpu.CompilerParams(
            dimension_semantics=("arbitrary",), vmem_limit_bytes=VMEM_LIMIT),
        name="mem_kv",
    )(mem, g_mem.reshape(1, d), w_xkv.astype(BF16))


def _mid_kernel(x_ref, ya_ref, ysb_ref, wo_ref, gx_ref, wq_ref, mkt_ref, mv_ref, wxo_ref,
                gf_ref, wr_ref, br_ref,
                x2_ref, h3_ref, meta_ref, cnt_ref, base_ref, o_ref):
    tm, d = x_ref.shape[1], x_ref.shape[2]
    dh = d // XA_HEADS
    cw = CONV_WIDTH
    first_step = jnp.logical_and(pl.program_id(0) == 0, pl.program_id(1) == 0)

    x1 = x_ref[0] + _dot(ya_ref[0], wo_ref[0:cw, :]) + _dot(ysb_ref[0], wo_ref[cw:2 * cw, :])

    hq = _rms(x1, gx_ref[...]).astype(BF16)
    qx = (_dot(hq, wq_ref[...]) * (dh ** -0.5)).astype(BF16)
    for h in range(XA_HEADS):
        s = _dot(qx[:, h * dh:(h + 1) * dh], mkt_ref[0, h * dh:(h + 1) * dh, :])
        e = jnp.exp(s - jnp.max(s, axis=-1, keepdims=True))
        oh = _dot(e.astype(BF16), mv_ref[0, :, h * dh:(h + 1) * dh])
        o_ref[:, h * dh:(h + 1) * dh] = (oh / jnp.sum(e, axis=-1, keepdims=True)).astype(BF16)
    x2 = x1 + _dot(o_ref[...], wxo_ref[...])
    x2_ref[0] = x2

    h3 = _rms(x2, gf_ref[...])
    h3_ref[0] = h3

    h_hi = h3.astype(BF16)
    h_lo = (h3 - h_hi.astype(F32)).astype(BF16)
    wr = wr_ref[...]
    w_hi = wr.astype(BF16)
    w_lo = (wr - w_hi.astype(F32)).astype(BF16)
    logits = _dot(h_hi, w_hi) + _dot(h_lo, w_hi) + _dot(h_hi, w_lo) + br_ref[...]

    lane = lax.broadcasted_iota(jnp.int32, (tm, LANES), 1)
    neg = jnp.float32(-jnp.inf)
    work = jnp.where(lane < N_EXPERTS, logits, neg)
    vals, idxs = [], []
    for _ in range(TOP_K):
        m = jnp.max(work, axis=-1, keepdims=True)
        idx = jnp.min(jnp.where(work == m, lane, LANES), axis=-1, keepdims=True)
        vals.append(m)
        idxs.append(idx)
        work = jnp.where(lane == idx, neg, work)
    exps = [jnp.exp(v - vals[0]) for v in vals]
    denom = exps[0] + exps[1] + exps[2] + exps[3]

    @pl.when(first_step)
    def _():
        base_ref[...] = jnp.zeros((1, LANES), F32)

    onehot = jnp.zeros((tm, LANES), F32)
    for idx in idxs:
        onehot = onehot + (lane == idx).astype(F32)
    r = lax.broadcasted_iota(jnp.int32, (tm, tm), 0)
    c = lax.broadcasted_iota(jnp.int32, (tm, tm), 1)
    before = (c < r).astype(BF16)
    rank_all = _dot(before, onehot.astype(BF16)) + base_ref[...]
    new_base = base_ref[...] + jnp.sum(onehot, axis=0, keepdims=True)
    base_ref[...] = new_base
    cnt_ref[...] = new_base

    meta = jnp.zeros((tm, LANES), F32)
    for k in range(TOP_K):
        rank_k = jnp.sum(jnp.where(lane == idxs[k], rank_all, 0.0), axis=-1, keepdims=True)
        meta = jnp.where(lane == k, idxs[k].astype(F32), meta)
        meta = jnp.where(lane == TOP_K + k, exps[k] / denom, meta)
        meta = jnp.where(lane == 2 * TOP_K + k, rank_k, meta)
    meta_ref[0] = meta


def _mid(x, ya, ysb, w_out, g_xattn, w_xq, mkt, mv, w_xo, g_ffn, w_router, b_router):
    bsz, seq, d = x.shape
    n_mem = mv.shape[1]
    tm = TM_MID
    wr = jnp.zeros((d, LANES), F32).at[:, :N_EXPERTS].set(w_router)
    br = jnp.zeros((1, LANES), F32).at[0, :N_EXPERTS].set(b_router)
    tile = lambda w: pl.BlockSpec((1, tm, w), lambda b, i: (b, i, 0))
    full = lambda *s: pl.BlockSpec(s, lambda b, i: (0,) * len(s))
    return pl.pallas_call(
        _mid_kernel,
        grid=(bsz, seq // tm),
        in_specs=[
            tile(d), tile(CONV_WIDTH), tile(SB_WIDTH),
            full(d, d), full(1, d), full(d, d),
            pl.BlockSpec((1, d, n_mem), lambda b, i: (b, 0, 0)),
            pl.BlockSpec((1, n_mem, d), lambda b, i: (b, 0, 0)),
            full(d, d), full(1, d), full(d, LANES), full(1, LANES),
        ],
        out_specs=[tile(d), tile(d), tile(LANES), full(1, LANES)],
        out_shape=[
            jax.ShapeDtypeStruct((bsz, seq, d), F32),
            jax.ShapeDtypeStruct((bsz, seq, d), F32),
            jax.ShapeDtypeStruct((bsz, seq, LANES), F32),
            jax.ShapeDtypeStruct((1, LANES), F32),
        ],
        scratch_shapes=[pltpu.VMEM((1, LANES), F32), pltpu.VMEM((tm, d), BF16)],
        compiler_params=pltpu.CompilerParams(
            dimension_semantics=("arbitrary", "arbitrary"), vmem_limit_bytes=VMEM_LIMIT),
        name="mid",
    )(x, ya, ysb, w_out.astype(BF16), g_xattn.reshape(1, d), w_xq.astype(BF16), mkt, mv,
      w_xo.astype(BF16), g_ffn.reshape(1, d), wr, br)


def _moe_kernel(be_ref, tok_ref, nused_ref,
                h_hbm, wgu_ref, bg_ref, bu_ref, wd_ref, bd_ref,
                y_ref, xbuf, wg_s, wu_s, wd_s, sem):
    m = pl.program_id(0)
    tm, d = xbuf.shape
    f = wd_ref.shape[1]

    @pl.when(m >= nused_ref[0])
    def _():
        y_ref[...] = jnp.zeros(y_ref.shape, F32)

    @pl.when(m < nused_ref[0])
    def _():
        def issue(r, c):
            t = tok_ref[m * tm + r]
            pltpu.make_async_copy(h_hbm.at[pl.ds(t, 1)], xbuf.at[pl.ds(r, 1)], sem).start()
            return c

        lax.fori_loop(0, tm, issue, 0, unroll=8)

        @pl.when(jnp.logical_or(m == 0, be_ref[m] != be_ref[jnp.maximum(m - 1, 0)]))
        def _():
            rr = lax.broadcasted_iota(jnp.int32, (2 * LANES, LANES), 0)
            cc = lax.broadcasted_iota(jnp.int32, (2 * LANES, LANES), 1)
            sel_g = (rr == 2 * cc).astype(BF16)
            sel_u = (rr == 2 * cc + 1).astype(BF16)
            for c in range(f // LANES):
                w = wgu_ref[0, :, 2 * LANES * c:2 * LANES * (c + 1)].astype(BF16)
                wg_s[:, LANES * c:LANES * (c + 1)] = _dot(w, sel_g).astype(BF16)
                wu_s[:, LANES * c:LANES * (c + 1)] = _dot(w, sel_u).astype(BF16)
            wd_s[...] = wd_ref[0].astype(BF16)

        pltpu.make_async_copy(h_hbm.at[pl.ds(0, tm)], xbuf, sem).wait()

        xb = xbuf[...].astype(BF16)
        g = jnp.minimum(_dot(xb, wg_s[...]) + bg_ref[0], SWIGLU_LIMIT)
        u = jnp.clip(_dot(xb, wu_s[...]) + bu_ref[0], -SWIGLU_LIMIT, SWIGLU_LIMIT)
        glu = g * jax.nn.sigmoid(g * SWIGLU_ALPHA)
        hid = ((u + 1.0) * glu).astype(BF16)
        y_ref[...] = _dot(hid, wd_s[...]) + bd_ref[0]


def _moe_experts(h3, block_e, row_tok, nused, w_gate_up, b_gate, b_up, w_down, b_down):
    n_tok, d = h3.shape
    n_blocks = block_e.shape[0]
    tm = TM_MOE
    f = w_down.shape[1]
    grid_spec = pltpu.PrefetchScalarGridSpec(
        num_scalar_prefetch=3,
        grid=(n_blocks,),
        in_specs=[
            pl.BlockSpec(memory_space=pl.ANY),
            pl.BlockSpec((1, d, 2 * f), lambda m, be, tok, nu: (be[m], 0, 0)),
            pl.BlockSpec((1, 1, f), lambda m, be, tok, nu: (be[m], 0, 0)),
            pl.BlockSpec((1, 1, f), lambda m, be, tok, nu: (be[m], 0, 0)),
            pl.BlockSpec((1, f, d), lambda m, be, tok, nu: (be[m], 0, 0)),
            pl.BlockSpec((1, 1, d), lambda m, be, tok, nu: (be[m], 0, 0)),
        ],
        out_specs=pl.BlockSpec((tm, d), lambda m, be, tok, nu: (m, 0)),
        scratch_shapes=[
            pltpu.VMEM((tm, d), F32),
            pltpu.VMEM((d, f), BF16),
            pltpu.VMEM((d, f), BF16),
            pltpu.VMEM((f, d), BF16),
            pltpu.SemaphoreType.DMA(()),
        ],
    )
    return pl.pallas_call(
        _moe_kernel,
        grid_spec=grid_spec,
        out_shape=jax.ShapeDtypeStruct((n_blocks * tm, d), F32),
        compiler_params=pltpu.CompilerParams(
            dimension_semantics=("arbitrary",), vmem_limit_bytes=VMEM_LIMIT),
        name="moe_experts",
    )(block_e, row_tok, nused, h3, w_gate_up,
      b_gate.reshape(N_EXPERTS, 1, f), b_up.reshape(N_EXPERTS, 1, f),
      w_down, b_down.reshape(N_EXPERTS, 1, d))


def _combine_kernel(pos_ref, x2_ref, meta_ref, gfin_ref, y_hbm, o_ref, ybuf, sem):
    i = pl.program_id(0)
    tm = x2_ref.shape[0]

    def issue(r, c):
        for k in range(TOP_K):
            p = pos_ref[(i * tm + r) * TOP_K + k]
            pltpu.make_async_copy(y_hbm.at[pl.ds(p, 1)], ybuf.at[k, pl.ds(r, 1)], sem).start()
        return c

    lax.fori_loop(0, tm, issue, 0, unroll=4)
    for k in range(TOP_K):
        pltpu.make_async_copy(y_hbm.at[pl.ds(0, tm)], ybuf.at[k], sem).wait()

    meta = meta_ref[...]
    x3 = x2_ref[...]
    for k in range(TOP_K):
        x3 = x3 + ybuf[k] * meta[:, TOP_K + k:TOP_K + k + 1]
    o_ref[...] = _rms(x3, gfin_ref[...])


def _combine(pos, x2, meta, g_final, y):
    n_tok, d = x2.shape
    tm = TM_OUT
    grid_spec = pltpu.PrefetchScalarGridSpec(
        num_scalar_prefetch=1,
        grid=(n_tok // tm,),
        in_specs=[
            pl.BlockSpec((tm, d), lambda i, pos: (i, 0)),
            pl.BlockSpec((tm, LANES), lambda i, pos: (i, 0)),
            pl.BlockSpec((1, d), lambda i, pos: (0, 0)),
            pl.BlockSpec(memory_space=pl.ANY),
        ],
        out_specs=pl.BlockSpec((tm, d), lambda i, pos: (i, 0)),
        scratch_shapes=[pltpu.VMEM((TOP_K, tm, d), F32), pltpu.SemaphoreType.DMA(())],
    )
    return pl.pallas_call(
        _combine_kernel,
        grid_spec=grid_spec,
        out_shape=jax.ShapeDtypeStruct((n_tok, d), F32),
        compiler_params=pltpu.CompilerParams(
            dimension_semantics=("arbitrary",), vmem_limit_bytes=VMEM_LIMIT),
        name="combine",
    )(pos, x2, meta, g_final.reshape(1, d), y)


def _routing(meta, counts):
    n_tok = meta.shape[0]
    tm = TM_MOE
    e = meta[:, 0:TOP_K].astype(jnp.int32)
    rank = meta[:, 2 * TOP_K:3 * TOP_K].astype(jnp.int32)
    cnt = counts[0, :N_EXPERTS].astype(jnp.int32)
    padded = (cnt + tm - 1) // tm * tm
    padded_end = jnp.cumsum(padded)
    padded_start = padded_end - padded
    pos = (padded_start[e] + rank).reshape(-1)
    n_blocks = n_tok * TOP_K // tm + N_EXPERTS
    tok = jnp.arange(n_tok * TOP_K, dtype=jnp.int32) // TOP_K
    row_tok = jnp.zeros((n_blocks * tm,), jnp.int32).at[pos].set(tok)
    block_e = jnp.minimum(
        jnp.searchsorted(padded_end, jnp.arange(n_blocks, dtype=jnp.int32) * tm, side="right"),
        N_EXPERTS - 1).astype(jnp.int32)
    nused = (padded_end[-1:] // tm).astype(jnp.int32)
    return pos, row_tok, block_e, nused


def kernel(x, mem, g_mix, w_in, w_conv, g_conv_out, g_sb_out, w_out, g_xattn, g_mem, w_xq, w_xkv, w_xo, g_ffn, w_router, b_router, w_gate_up, b_gate_up, w_down, b_down, g_final):
    bsz, seq, d = x.shape
    depth = g_mix.shape[0]
    for l in range(depth):
        ya, q, kt, v = _mixer_in(x, g_mix[l], w_in[l], w_conv[l], g_conv_out[l])
        ysb = _sb_attention(q, kt, v, g_sb_out[l])
        mkt, mv = _mem_kv(mem, g_mem[l], w_xkv[l])
        x2, h3, meta, counts = _mid(x, ya, ysb, w_out[l], g_xattn[l], w_xq[l], mkt, mv, w_xo[l],
                                    g_ffn[l], w_router[l], b_router[l])
        n_tok = bsz * seq
        meta = meta.reshape(n_tok, LANES)
        pos, row_tok, block_e, nused = _routing(meta, counts)
        y = _moe_experts(h3.reshape(n_tok, d), block_e, row_tok, nused, w_gate_up[l],
                         b_gate_up[l][:, 0::2], b_gate_up[l][:, 1::2], w_down[l], b_down[l])
        assert l == depth - 1, "combine applies the final norm; deeper stacks need an un-normed variant"
        x = _combine(pos, x2.reshape(n_tok, d), meta, g_final, y).reshape(bsz, seq, d)
    return x
```
